```python
import math
import jax, jax.numpy as jnp
from jax import lax
import numpy as np

D_MODEL = 2048
BATCH = 2
SEQ = 16384
DEPTH = 2
DEC_BATCH = 32
DEC_SEQ = 16
PAST_LEN = 2048

CHUNK = 64
Q_BLOCK = 128
HEAD_DIM = 128
N_MIX_HEADS = D_MODEL // HEAD_DIM
H_FOX = N_MIX_HEADS // 2
H_BAND = N_MIX_HEADS - H_FOX
ATT_SCALE = HEAD_DIM ** -0.5
BAND_PREV_CHUNKS = 8
BAND_KEEP = BAND_PREV_CHUNKS * CHUNK
BAND_LEN = BAND_KEEP + CHUNK
REL_CLIP = 128
FOX_BIAS_INIT = 3.0
EVEN_IN = 3 * H_FOX * HEAD_DIM + H_FOX + 3 * H_BAND * HEAD_DIM
H_MLA = 16
Q_LORA = 512
KV_LORA = 512
QK_NOPE = 128
QK_ROPE = 64
V_HEAD = 128
MLA_SCALE = (QK_NOPE + QK_ROPE) ** -0.5
ROPE_THETA = 10000.0
ODD_IN = Q_LORA + KV_LORA + QK_ROPE
PEER_HEADS = 8
N_KEYS = 128
N_EXPERTS = N_KEYS * N_KEYS
PEER_TOPK = 16
PEER_QDIM = 256
PEER_HALF = PEER_QDIM // 2
PEER_BLOCK = 128
N_EVEN = (DEPTH + 1) // 2
N_ODD = DEPTH // 2
EPS = 1e-6

kernel_name = 'hybrid_fox_band_mla_peer_stream_step'


def rmsnorm(x, g):
    x32 = x.astype(jnp.float32)
    y = x32 * lax.rsqrt(jnp.mean(x32 * x32, axis=-1, keepdims=True) + EPS)
    return (y * g.astype(jnp.float32)).astype(x.dtype)


def ada_mod(c, w_ada, b_ada):
    m = jax.nn.silu(c) @ w_ada + b_ada
    return jnp.split(m[:, None, :], 6, axis=-1)


def modulate(h, shift, scale):
    return h * (1 + scale) + shift


def rope(x, pos):
    half = QK_ROPE // 2
    freqs = ROPE_THETA ** (-jnp.arange(half, dtype=jnp.float32) / half)
    ang = pos.astype(jnp.float32)[..., None] * freqs
    cos, sin = jnp.cos(ang), jnp.sin(ang)
    x1 = x[..., :half].astype(jnp.float32)
    x2 = x[..., half:].astype(jnp.float32)
    return jnp.concatenate([x1 * cos - x2 * sin, x2 * cos + x1 * sin], axis=-1).astype(x.dtype)


def to_blocks(a, blk):
    return jnp.swapaxes(a.reshape((a.shape[0], a.shape[1] // blk, blk) + a.shape[2:]), 0, 1)


def from_blocks(a):
    a = jnp.swapaxes(a, 0, 1)
    return a.reshape((a.shape[0], a.shape[1] * a.shape[2]) + a.shape[3:])


def fox_core(q, k, v, fq, fk, qpos, kpos):
    s = jnp.einsum('bqhd,bkhd->bhqk', q, k, preferred_element_type=jnp.float32) * ATT_SCALE
    s = s + (jnp.swapaxes(fq, 1, 2)[..., :, None] - jnp.swapaxes(fk, 1, 2)[..., None, :])
    s = jnp.where(kpos[None, :] <= qpos[:, None], s, -jnp.inf)
    p = jax.nn.softmax(s, axis=-1)
    return jnp.einsum('bhqk,bkhd->bqhd', p.astype(v.dtype), v)


def fox_prompt(q, k, v, logf):
    S = q.shape[1]
    F = jnp.cumsum(logf, axis=1)
    pos = jnp.arange(S)
    qb, fb = to_blocks(q, Q_BLOCK), to_blocks(F, Q_BLOCK)
    pb = pos.reshape(S // Q_BLOCK, Q_BLOCK)
    out = lax.map(lambda a: fox_core(a[0], k, v, a[1], F, a[2], pos), (qb, fb, pb))
    return from_blocks(out)


def fox_sample(q, k, v, logf, ck, cv, clogf):
    P, T = ck.shape[1], q.shape[1]
    k_all = jnp.concatenate([ck, k], axis=1)
    v_all = jnp.concatenate([cv, v], axis=1)
    F = jnp.cumsum(jnp.concatenate([clogf.astype(jnp.float32), logf], axis=1), axis=1)
    return fox_core(q, k_all, v_all, F[:, P:], F, P + jnp.arange(T), jnp.arange(P + T))


def rel_bias(table, d):
    idx = jnp.clip(d, -REL_CLIP, REL_CLIP) + REL_CLIP
    return jnp.take(table, idx, axis=1).astype(jnp.float32)


def band_core(q, k, v, bias, valid):
    s = jnp.einsum('bqhd,bkhd->bhqk', q, k, preferred_element_type=jnp.float32) * ATT_SCALE + bias[None]
    s = jnp.where(valid[None, None, None, :], s, -jnp.inf)
    p = jax.nn.softmax(s, axis=-1)
    return jnp.einsum('bhqk,bkhd->bqhd', p.astype(v.dtype), v)


def band_prompt(q, k, v, table):
    S = q.shape[1]
    kp = jnp.pad(k, ((0, 0), (BAND_KEEP, 0), (0, 0), (0, 0)))
    vp = jnp.pad(v, ((0, 0), (BAND_KEEP, 0), (0, 0), (0, 0)))
    qi, ri = jnp.arange(CHUNK), jnp.arange(BAND_LEN)
    bias = rel_bias(table, (qi[:, None] + BAND_KEEP) - ri[None, :])

    def one(a):
        qc, c = a
        start = c * CHUNK
        kb = lax.dynamic_slice_in_dim(kp, start, BAND_LEN, axis=1)
        vb = lax.dynamic_slice_in_dim(vp, start, BAND_LEN, axis=1)
        valid = start + ri >= BAND_KEEP
        return band_core(qc, kb, vb, bias, valid)

    out = lax.map(one, (to_blocks(q, CHUNK), jnp.arange(S // CHUNK)))
    return from_blocks(out)


def band_sample(q, k, v, ck, cv, table):
    L, T = ck.shape[1], q.shape[1]
    k_all = jnp.concatenate([ck, k], axis=1)
    v_all = jnp.concatenate([cv, v], axis=1)
    bias = rel_bias(table, (L + jnp.arange(T))[:, None] - jnp.arange(L + T)[None, :])
    return band_core(q, k_all, v_all, bias, jnp.ones((L + T,), dtype=bool))


def even_project(h, w_in, b_f):
    B, T, _ = h.shape
    a, bw = H_FOX * HEAD_DIM, H_BAND * HEAD_DIM
    z = h @ w_in
    q_a, k_a, v_a, f, q_b, k_b, v_b = jnp.split(
        z, [a, 2 * a, 3 * a, 3 * a + H_FOX, 3 * a + H_FOX + bw, 3 * a + H_FOX + 2 * bw], axis=-1)
    heads = lambda t, n: t.reshape(B, T, n, HEAD_DIM)
    logf = jax.nn.log_sigmoid(f.astype(jnp.float32) + b_f.astype(jnp.float32))
    return (heads(q_a, H_FOX), heads(k_a, H_FOX), heads(v_a, H_FOX), logf,
            heads(q_b, H_BAND), heads(k_b, H_BAND), heads(v_b, H_BAND))


def merge_even(o_a, o_b, w_out):
    B, T = o_a.shape[:2]
    return jnp.concatenate([o_a.reshape(B, T, -1), o_b.reshape(B, T, -1)], axis=-1) @ w_out


def odd_project(h, pos, w_in, g_q, g_kv, w_uq):
    B, T, _ = h.shape
    cq, ckv, kr = jnp.split(h @ w_in, [Q_LORA, Q_LORA + KV_LORA], axis=-1)
    cq, ckv = rmsnorm(cq, g_q), rmsnorm(ckv, g_kv)
    q = (cq @ w_uq).reshape(B, T, H_MLA, QK_NOPE + QK_ROPE)
    return q[..., :QK_NOPE], rope(q[..., QK_NOPE:], pos[:, None]), ckv, rope(kr, pos)


def mla_attend(q_nope, q_rope, ckv, krope, qpos, kpos, w_uk, w_uv):
    q_lat = jnp.einsum('bqhd,chd->bqhc', q_nope, w_uk)
    s = (jnp.einsum('bqhc,bkc->bhqk', q_lat, ckv, preferred_element_type=jnp.float32)
         + jnp.einsum('bqhr,bkr->bhqk', q_rope, krope, preferred_element_type=jnp.float32)) * MLA_SCALE
    s = jnp.where((kpos // CHUNK)[None, :] <= (qpos // CHUNK)[:, None], s, -jnp.inf)
    p = jax.nn.softmax(s, axis=-1)
    o_lat = jnp.einsum('bhqk,bkc->bqhc', p.astype(ckv.dtype), ckv)
    return jnp.einsum('bqhc,chd->bqhd', o_lat, w_uv)


def mla_prompt(q_nope, q_rope, ckv, krope, w_uk, w_uv):
    S = q_nope.shape[1]
    pos = jnp.arange(S)
    pb = pos.reshape(S // Q_BLOCK, Q_BLOCK)
    out = lax.map(lambda a: mla_attend(a[0], a[1], ckv, krope, a[2], pos, w_uk, w_uv),
                  (to_blocks(q_nope, Q_BLOCK), to_blocks(q_rope, Q_BLOCK), pb))
    return from_blocks(out)


def mla_sample(q_nope, q_rope, ckv, krope, c_ckv, c_krope, w_uk, w_uv):
    P, T = c_ckv.shape[1], q_nope.shape[1]
    ckv_all = jnp.concatenate([c_ckv, ckv], axis=1)
    kr_all = jnp.concatenate([c_krope, krope], axis=1)
    return mla_attend(q_nope, q_rope, ckv_all, kr_all, P + jnp.arange(T), jnp.arange(P + T), w_uk, w_uv)


def peer_block(x, w_q, key1, key2, u, vv):
    n = x.shape[0]
    q = (x @ w_q).reshape(n, PEER_HEADS, PEER_QDIM)
    s1 = jnp.einsum('nhd,hkd->nhk', q[..., :PEER_HALF], key1, preferred_element_type=jnp.float32)
    s2 = jnp.einsum('nhd,hkd->nhk', q[..., PEER_HALF:], key2, preferred_element_type=jnp.float32)
    v1, i1 = lax.top_k(s1, PEER_TOPK)
    v2, i2 = lax.top_k(s2, PEER_TOPK)
    cand = (v1[..., :, None] + v2[..., None, :]).reshape(n, PEER_HEADS, PEER_TOPK * PEER_TOPK)
    cidx = (i1[..., :, None] * N_KEYS + i2[..., None, :]).reshape(n, PEER_HEADS, PEER_TOPK * PEER_TOPK)
    sc, sel = lax.top_k(cand, PEER_TOPK)
    eidx = jnp.take_along_axis(cidx, sel, axis=-1)
    g = jax.nn.softmax(sc, axis=-1)
    ue = jnp.take(u, eidx, axis=0)
    act = jax.nn.gelu(jnp.einsum('nd,nhkd->nhk', x, ue, preferred_element_type=jnp.float32), approximate=False)
    ve = jnp.take(vv, eidx, axis=0)
    return jnp.einsum('nhk,nhkd->nd', (g * act).astype(x.dtype), ve)


def peer_ffn(h, w_q, key1, key2, u, vv):
    B, T, D = h.shape
    n = B * T
    blk = min(PEER_BLOCK, n)
    nb = -(-n // blk)
    flat = jnp.pad(h.reshape(n, D), ((0, nb * blk - n), (0, 0))).reshape(nb, blk, D)
    out = lax.map(lambda xb: peer_block(xb, w_q, key1, key2, u, vv), flat)
    return out.reshape(nb * blk, D)[:n].reshape(B, T, D)


def setup_inputs(seed: int = 0) -> dict:
    key = jax.random.key(seed)
    ks = iter(jax.random.split(key, 48))
    nrm = lambda shape, scale: jax.random.normal(next(ks), shape, jnp.float32) * scale
    gain = lambda shape: jnp.ones(shape, jnp.float32) + nrm(shape, 0.01)
    band_keep = min(BAND_KEEP, PAST_LEN)
    D = D_MODEL
    return {
        'x_prompt': nrm((BATCH, SEQ, D), 1.0),
        'x_sample': nrm((DEC_BATCH, DEC_SEQ, D), 1.0),
        'cache_fox_k': nrm((N_EVEN, DEC_BATCH, PAST_LEN, H_FOX, HEAD_DIM), 1.0),
        'cache_fox_v': nrm((N_EVEN, DEC_BATCH, PAST_LEN, H_FOX, HEAD_DIM), 1.0),
        'cache_fox_logf': jax.nn.log_sigmoid(FOX_BIAS_INIT + nrm((N_EVEN, DEC_BATCH, PAST_LEN, H_FOX), 1.0)),
        'cache_band_k': nrm((N_EVEN, DEC_BATCH, band_keep, H_BAND, HEAD_DIM), 1.0),
        'cache_band_v': nrm((N_EVEN, DEC_BATCH, band_keep, H_BAND, HEAD_DIM), 1.0),
        'cache_mla_ckv': nrm((N_ODD, DEC_BATCH, PAST_LEN, KV_LORA), 1.0),
        'cache_mla_krope': nrm((N_ODD, DEC_BATCH, PAST_LEN, QK_ROPE), 1.0),
        'c_prompt': nrm((BATCH, D), 1.0),
        'c_sample': nrm((DEC_BATCH, D), 1.0),
        'w_ada': nrm((DEPTH, D, 6 * D), 0.5 * D ** -0.5),
        'b_ada': nrm((DEPTH, 6 * D), 0.01),
        'g_mix': gain((DEPTH, D)),
        'g_ffn': gain((DEPTH, D)),
        'w_even_in': nrm((N_EVEN, D, EVEN_IN), D ** -0.5),
        'b_forget': FOX_BIAS_INIT + nrm((N_EVEN, H_FOX), 0.1),
        'rel_bias_table': nrm((N_EVEN, H_BAND, 2 * REL_CLIP + 1), 0.5),
        'w_even_out': nrm((N_EVEN, N_MIX_HEADS * HEAD_DIM, D), (N_MIX_HEADS * HEAD_DIM) ** -0.5),
        'w_odd_in': nrm((N_ODD, D, ODD_IN), D ** -0.5),
        'g_q_lora': gain((N_ODD, Q_LORA)),
        'g_kv_lora': gain((N_ODD, KV_LORA)),
        'w_uq': nrm((N_ODD, Q_LORA, H_MLA * (QK_NOPE + QK_ROPE)), Q_LORA ** -0.5),
        'w_uk': nrm((N_ODD, KV_LORA, H_MLA, QK_NOPE), KV_LORA ** -0.5),
        'w_uv': nrm((N_ODD, KV_LORA, H_MLA, V_HEAD), KV_LORA ** -0.5),
        'w_odd_out': nrm((N_ODD, H_MLA * V_HEAD, D), (H_MLA * V_HEAD) ** -0.5),
        'w_peer_q': nrm((DEPTH, D, PEER_HEADS * PEER_QDIM), D ** -0.5),
        'peer_key1': nrm((DEPTH, PEER_HEADS, N_KEYS, PEER_HALF), PEER_HALF ** -0.5),
        'peer_key2': nrm((DEPTH, PEER_HEADS, N_KEYS, PEER_HALF), PEER_HALF ** -0.5),
        'peer_u': nrm((DEPTH, N_EXPERTS, D), D ** -0.5),
        'peer_v': nrm((DEPTH, N_EXPERTS, D), PEER_HEADS ** -0.5),
        'g_final': gain((D,)),
    }


def reference(x_prompt, x_sample, cache_fox_k, cache_fox_v, cache_fox_logf, cache_band_k, cache_band_v,
              cache_mla_ckv, cache_mla_krope, c_prompt, c_sample, w_ada, b_ada, g_mix, g_ffn,
              w_even_in, b_forget, rel_bias_table, w_even_out, w_odd_in, g_q_lora, g_kv_lora,
              w_uq, w_uk, w_uv, w_odd_out, w_peer_q, peer_key1, peer_key2, peer_u, peer_v, g_final):
    S, T = x_prompt.shape[1], x_sample.shape[1]
    P = cache_fox_k.shape[2]
    pos_p, pos_s = jnp.arange(S), P + jnp.arange(T)
    keep = min(BAND_KEEP, S)
    pfk, pfv, pfl, pbk, pbv, pck, pkr = [], [], [], [], [], [], []
    sfk, sfv, sfl, sbk, sbv, sck, skr = [], [], [], [], [], [], []
    xp, xs = x_prompt, x_sample
    for l in range(DEPTH):
        mp = ada_mod(c_prompt, w_ada[l], b_ada[l])
        ms = ada_mod(c_sample, w_ada[l], b_ada[l])
        hp = modulate(rmsnorm(xp, g_mix[l]), mp[0], mp[1])
        hs = modulate(rmsnorm(xs, g_mix[l]), ms[0], ms[1])
        if l % 2 == 0:
            e = l // 2
            qa, ka, va, lfa, qb, kb, vb = even_project(hp, w_even_in[e], b_forget[e])
            op = merge_even(fox_prompt(qa, ka, va, lfa), band_prompt(qb, kb, vb, rel_bias_table[e]), w_even_out[e])
            pfk.append(ka); pfv.append(va); pfl.append(lfa)
            pbk.append(kb[:, S - keep:]); pbv.append(vb[:, S - keep:])
            qa, ka, va, lfa, qb, kb, vb = even_project(hs, w_even_in[e], b_forget[e])
            o_a = fox_sample(qa, ka, va, lfa, cache_fox_k[e], cache_fox_v[e], cache_fox_logf[e])
            o_b = band_sample(qb, kb, vb, cache_band_k[e], cache_band_v[e], rel_bias_table[e])
            os_ = merge_even(o_a, o_b, w_even_out[e])
            sfk.append(ka); sfv.append(va); sfl.append(lfa); sbk.append(kb); sbv.append(vb)
        else:
            o = l // 2
            qn, qr, ckv, kr = odd_project(hp, pos_p, w_odd_in[o], g_q_lora[o], g_kv_lora[o], w_uq[o])
            op = mla_prompt(qn, qr, ckv, kr, w_uk[o], w_uv[o]).reshape(xp.shape[0], S, -1) @ w_odd_out[o]
            pck.append(ckv); pkr.append(kr)
            qn, qr, ckv, kr = odd_project(hs, pos_s, w_odd_in[o], g_q_lora[o], g_kv_lora[o], w_uq[o])
            os_ = mla_sample(qn, qr, ckv, kr, cache_mla_ckv[o], cache_mla_krope[o],
                             w_uk[o], w_uv[o]).reshape(xs.shape[0], T, -1) @ w_odd_out[o]
            sck.append(ckv); skr.append(kr)
        xp = xp + mp[2] * op
        xs = xs + ms[2] * os_
        hp = modulate(rmsnorm(xp, g_ffn[l]), mp[3], mp[4])
        hs = modulate(rmsnorm(xs, g_ffn[l]), ms[3], ms[4])
        xp = xp + mp[5] * peer_ffn(hp, w_peer_q[l], peer_key1[l], peer_key2[l], peer_u[l], peer_v[l])
        xs = xs + ms[5] * peer_ffn(hs, w_peer_q[l], peer_key1[l], peer_key2[l], peer_u[l], peer_v[l])
    y_prompt = rmsnorm(xp, g_final)
    y_sample = rmsnorm(xs, g_final)
    return (y_prompt, y_sample,
            jnp.stack(pfk), jnp.stack(pfv), jnp.stack(pfl), jnp.stack(pbk), jnp.stack(pbv),
            jnp.stack(pck), jnp.stack(pkr),
            jnp.stack(sfk), jnp.stack(sfv), jnp.stack(sfl), jnp.stack(sbk), jnp.stack(sbv),
            jnp.stack(sck), jnp.stack(skr))
```

```python
import functools
import math

import jax
import jax.numpy as jnp
import numpy as np
from jax import lax
from jax.experimental import pallas as pl
from jax.experimental.pallas import tpu as pltpu

F32 = jnp.float32
BF16 = jnp.bfloat16

HEAD_DIM = 128
H_FOX = 8
H_BAND = 8
CHUNK = 64
BAND_KEEP = 512
REL_CLIP = 128
H_MLA = 16
Q_LORA = 512
KV_LORA = 512
QK_NOPE = 128
QK_ROPE = 64
V_HEAD = 128
ROPE_THETA = 10000.0
PEER_HEADS = 8
N_KEYS = 128
PEER_TOPK = 16
PEER_HALF = 128
EPS = 1e-6
ATT_SCALE = HEAD_DIM ** -0.5
MLA_SCALE = (QK_NOPE + QK_ROPE) ** -0.5

LANES = 128
VMEM_LIMIT_BYTES = 48 * 1024 * 1024
PEER_VMEM_LIMIT_BYTES = 56 * 1024 * 1024

NT_DIMS = (((1,), (1,)), ((), ()))
NEG_INF = float("-inf")


def _params(n_axes, vmem_limit_bytes=VMEM_LIMIT_BYTES):
    return pltpu.CompilerParams(
        dimension_semantics=("arbitrary",) * n_axes, vmem_limit_bytes=vmem_limit_bytes)


def _row_spec(arr, tm, tn=None, col_axis=None):
    ch = arr.shape[2] if tn is None else tn
    per_row = arr.shape[1] != 1
    rows = tm if per_row else 1

    def index(*ids):
        b, i = ids[0], ids[1]
        j = 0 if col_axis is None else ids[col_axis]
        return (b, i if per_row else 0, j)

    return pl.BlockSpec((1, rows, ch), index)


def _ada_kernel(c_ref, w_ref, b_ref, o_ref):
    c = c_ref[...]
    x = c * jax.nn.sigmoid(c)
    o_ref[0] = jnp.dot(x.astype(BF16), w_ref[0].astype(BF16), preferred_element_type=F32) + b_ref[0]


def _ada(c, w_ada, b_ada):
    depth, d, n = w_ada.shape
    rows = c.shape[0]
    tn = 1024
    return pl.pallas_call(
        _ada_kernel,
        grid=(depth, n // tn),
        in_specs=[pl.BlockSpec((rows, d), lambda l, j: (0, 0)),
                  pl.BlockSpec((1, d, tn), lambda l, j: (l, 0, j)),
                  pl.BlockSpec((1, 1, tn), lambda l, j: (l, 0, j))],
        out_specs=pl.BlockSpec((1, rows, tn), lambda l, j: (l, 0, j)),
        out_shape=jax.ShapeDtypeStruct((depth, rows, n), F32),
        compiler_params=_params(2), name="ada_mod",
    )(c, w_ada, b_ada.reshape(depth, 1, n))


def _norm_kernel(x_ref, g_ref, *rest, modulate):
    o_ref = rest[-1]
    x = x_ref[0]
    y = x * lax.rsqrt(jnp.mean(x * x, axis=-1, keepdims=True) + EPS) * g_ref[...]
    if modulate:
        sh_ref, sc_ref = rest[0], rest[1]
        y = y * (1.0 + sc_ref[0]) + sh_ref[0]
    o_ref[0] = y.astype(o_ref.dtype)


def _norm(x, g, shift=None, scale=None, out_dtype=BF16):
    b, t, d = x.shape
    tm = min(512, t)
    modulate = shift is not None
    ins = [x, g.reshape(1, d)]
    specs = [pl.BlockSpec((1, tm, d), lambda bb, i: (bb, i, 0)), pl.BlockSpec((1, d), lambda bb, i: (0, 0))]
    if modulate:
        ins += [shift, scale]
        specs += [_row_spec(shift, tm), _row_spec(scale, tm)]
    return pl.pallas_call(
        functools.partial(_norm_kernel, modulate=modulate),
        grid=(b, t // tm), in_specs=specs,
        out_specs=pl.BlockSpec((1, tm, d), lambda bb, i: (bb, i, 0)),
        out_shape=jax.ShapeDtypeStruct((b, t, d), out_dtype),
        compiler_params=_params(2), name="rmsnorm_mod",
    )(*ins)


def _mm_kernel(*refs, n_in, gated):
    acc = None
    for x_ref, w_ref in zip(refs[:n_in], refs[n_in:2 * n_in]):
        d = jnp.dot(x_ref[0], w_ref[0], preferred_element_type=F32)
        acc = d if acc is None else acc + d
    rest = refs[2 * n_in:]
    if gated:
        res_ref, gate_ref, o_ref = rest
        o_ref[0] = res_ref[0] + gate_ref[0] * acc
    else:
        for o_ref in rest:
            o_ref[0] = acc.astype(o_ref.dtype)


def _pick_tn(n):
    for tn in (1024, 768, 640, 512, 384, 256, 128):
        if n % tn == 0:
            return tn
    raise ValueError(n)


def _mm(xs, ws, out_dtypes=(F32,), res=None, gate=None, name="matmul"):
    b, t, _ = xs[0].shape
    n = ws[0].shape[2]
    tm = min(1024, t)
    tn = _pick_tn(n)
    gated = res is not None
    in_specs = [pl.BlockSpec((1, tm, x.shape[2]), lambda bb, i, j: (bb, i, 0)) for x in xs]
    for w in ws:
        if w.shape[0] == 1:
            in_specs.append(pl.BlockSpec((1, w.shape[1], tn), lambda bb, i, j: (0, 0, j)))
        else:
            in_specs.append(pl.BlockSpec((1, w.shape[1], tn), lambda bb, i, j: (bb, 0, j)))
    ins = list(xs) + list(ws)
    if gated:
        ins += [res, gate]
        in_specs += [pl.BlockSpec((1, tm, tn), lambda bb, i, j: (bb, i, j)), _row_spec(gate, tm, tn, col_axis=2)]
        out_dtypes = (F32,)
    outs = pl.pallas_call(
        functools.partial(_mm_kernel, n_in=len(xs), gated=gated),
        grid=(b, t // tm, n // tn), in_specs=in_specs,
        out_specs=[pl.BlockSpec((1, tm, tn), lambda bb, i, j: (bb, i, j)) for _ in out_dtypes],
        out_shape=[jax.ShapeDtypeStruct((b, t, n), dt) for dt in out_dtypes],
        compiler_params=_params(3), name=name,
    )(*ins)
    return outs[0] if len(outs) == 1 else outs


def _logf_kernel(h_ref, w_ref, b_ref, o_ref):
    z = jnp.dot(h_ref[0], w_ref[...], preferred_element_type=F32) + b_ref[...]
    ls = -(jnp.maximum(-z, 0.0) + jnp.log1p(jnp.exp(-jnp.abs(z))))
    o_ref[0] = ls[:, :H_FOX]


def _logf(h, w_f, b_f):
    b, t, d = h.shape
    tm = min(1024, t)
    return pl.pallas_call(
        _logf_kernel, grid=(b, t // tm),
        in_specs=[pl.BlockSpec((1, tm, d), lambda bb, i: (bb, i, 0)),
                  pl.BlockSpec((d, LANES), lambda bb, i: (0, 0)),
                  pl.BlockSpec((1, LANES), lambda bb, i: (0, 0))],
        out_specs=pl.BlockSpec((1, tm, H_FOX), lambda bb, i: (bb, i, 0)),
        out_shape=jax.ShapeDtypeStruct((b, t, H_FOX), F32),
        compiler_params=_params(2), name="forget_gate",
    )(h, w_f, b_f)


def _cumsum_kernel(x_ref, c0_ref, o_ref, cend_ref, carry, *, ts):
    j = pl.program_id(1)

    @pl.when(j == 0)
    def _():
        carry[...] = c0_ref[0]

    r = lax.broadcasted_iota(jnp.int32, (ts, ts), 0)
    c = lax.broadcasted_iota(jnp.int32, (ts, ts), 1)
    upper = (r <= c).astype(F32)
    y = jnp.dot(x_ref[0], upper, precision=lax.Precision.HIGHEST, preferred_element_type=F32) + carry[...]
    o_ref[0] = y
    carry[...] = y[:, ts - 1:ts]
    cend_ref[0] = y[:, ts - 1:ts]


def _cumsum_rows(x, c0):
    b, h, t = x.shape
    ts = min(256, t)
    return pl.pallas_call(
        functools.partial(_cumsum_kernel, ts=ts), grid=(b, t // ts),
        in_specs=[pl.BlockSpec((1, h, ts), lambda bb, j: (bb, 0, j)),
                  pl.BlockSpec((1, h, 1), lambda bb, j: (bb, 0, 0))],
        out_specs=[pl.BlockSpec((1, h, ts), lambda bb, j: (bb, 0, j)),
                   pl.BlockSpec((1, h, 1), lambda bb, j: (bb, 0, 0))],
        out_shape=[jax.ShapeDtypeStruct((b, h, t), F32), jax.ShapeDtypeStruct((b, h, 1), F32)],
        scratch_shapes=[pltpu.VMEM((h, 1), F32)],
        compiler_params=_params(2), name="logf_cumsum",
    )(x, c0)


def _online_softmax_step(s, v, carry):
    m, l, acc = carry
    m_new = jnp.maximum(m, jnp.max(s, axis=1, keepdims=True))
    alpha = jnp.exp(m - m_new)
    p = jnp.exp(s - m_new)
    l = alpha * l + jnp.sum(p, axis=1, keepdims=True)
    acc = alpha * acc + jnp.dot(p.astype(BF16), v, preferred_element_type=F32)
    return m_new, l, acc


def _softmax_init(t, dv):
    return (jnp.full((t, 1), NEG_INF, F32), jnp.zeros((t, 1), F32), jnp.zeros((t, dv), F32))


def _tile(j, t):
    return pl.ds(pl.multiple_of(j * t, t), t)


def _fox_kernel(q_ref, k_ref, v_ref, frow_ref, fcol_ref, o_ref, *, t):
    i = pl.program_id(2)
    q = q_ref[0]
    fq = fcol_ref[0]

    def scores(j):
        k = k_ref[0, _tile(j, t), :]
        s = lax.dot_general(q, k, NT_DIMS, preferred_element_type=F32) * ATT_SCALE
        return s + (fq - frow_ref[0, :, _tile(j, t)])

    def body(j, carry):
        return _online_softmax_step(scores(j), v_ref[0, _tile(j, t), :], carry)

    carry = lax.fori_loop(0, i, body, _softmax_init(t, HEAD_DIM))
    row = lax.broadcasted_iota(jnp.int32, (t, t), 0)
    col = lax.broadcasted_iota(jnp.int32, (t, t), 1)
    s = jnp.where(col <= row, scores(i), NEG_INF)
    _, l, acc = _online_softmax_step(s, v_ref[0, _tile(i, t), :], carry)
    o_ref[0] = (acc / l).astype(o_ref.dtype)


def _fox_prompt(zb, frow, fcol):
    b, s, _ = zb.shape
    t = 512
    return pl.pallas_call(
        functools.partial(_fox_kernel, t=t), grid=(b, H_FOX, s // t),
        in_specs=[pl.BlockSpec((1, t, HEAD_DIM), lambda bb, h, i: (bb, i, h)),
                  pl.BlockSpec((1, s, HEAD_DIM), lambda bb, h, i: (bb, 0, H_FOX + h)),
                  pl.BlockSpec((1, s, HEAD_DIM), lambda bb, h, i: (bb, 0, 2 * H_FOX + h)),
                  pl.BlockSpec((1, 1, s), lambda bb, h, i: (bb * H_FOX + h, 0, 0)),
                  pl.BlockSpec((1, t, 1), lambda bb, h, i: (bb * H_FOX + h, i, 0))],
        out_specs=pl.BlockSpec((1, t, HEAD_DIM), lambda bb, h, i: (bb, i, h)),
        out_shape=jax.ShapeDtypeStruct((b, s, H_FOX * HEAD_DIM), BF16),
        compiler_params=_params(3), name="fox_prompt",
    )(zb, zb, zb, frow, fcol)


def _band_kernel(q_ref, k_ref, v_ref, bias_ref, o_ref, *, t):
    i = pl.program_id(2)
    q = q_ref[0]
    qc = lax.broadcasted_iota(jnp.int32, (t, t), 0) // CHUNK
    kc = lax.broadcasted_iota(jnp.int32, (t, t), 1) // CHUNK

    def tile(j, which, mask, carry):
        k = k_ref[0, _tile(j, t), :]
        s = lax.dot_general(q, k, NT_DIMS, preferred_element_type=F32) * ATT_SCALE + bias_ref[0, which]
        return _online_softmax_step(jnp.where(mask, s, NEG_INF), v_ref[0, _tile(j, t), :], carry)

    carry = tile(i, 0, kc <= qc, _softmax_init(t, HEAD_DIM))
    carry = lax.fori_loop(0, jnp.minimum(i, 1), lambda _, c: tile(i - 1, 1, kc >= qc, c), carry)
    _, l, acc = carry
    o_ref[0] = (acc / l).astype(o_ref.dtype)


def _band_prompt(zb, bias_tiles):
    b, s, _ = zb.shape
    t = BAND_KEEP
    base = 3 * H_FOX
    return pl.pallas_call(
        functools.partial(_band_kernel, t=t), grid=(b, H_BAND, s // t),
        in_specs=[pl.BlockSpec((1, t, HEAD_DIM), lambda bb, h, i: (bb, i, base + h)),
                  pl.BlockSpec((1, s, HEAD_DIM), lambda bb, h, i: (bb, 0, base + H_BAND + h)),
                  pl.BlockSpec((1, s, HEAD_DIM), lambda bb, h, i: (bb, 0, base + 2 * H_BAND + h)),
                  pl.BlockSpec((1, 2, t, t), lambda bb, h, i: (h, 0, 0, 0))],
        out_specs=pl.BlockSpec((1, t, HEAD_DIM), lambda bb, h, i: (bb, i, h)),
        out_shape=jax.ShapeDtypeStruct((b, s, H_BAND * HEAD_DIM), BF16),
        compiler_params=_params(3), name="band_prompt",
    )(zb, zb, zb, bias_tiles)


def _mla_kernel(qn_ref, qr_ref, kn_ref, kr_ref, v_ref, o_ref, *, t):
    h = pl.program_id(1)
    i = pl.program_id(2)
    lane = lax.broadcasted_iota(jnp.int32, (t, LANES), 1)
    mine = (lane // QK_ROPE) == (h % 2)
    qr = jnp.where(mine, qr_ref[0], jnp.zeros_like(qr_ref[0]))
    q = jnp.concatenate([qn_ref[0], qr], axis=1)

    def scores(j):
        k = jnp.concatenate([kn_ref[0, _tile(j, t), :], kr_ref[0, _tile(j, t), :]], axis=1)
        return lax.dot_general(q, k, NT_DIMS, preferred_element_type=F32) * MLA_SCALE

    def body(j, carry):
        return _online_softmax_step(scores(j), v_ref[0, _tile(j, t), :], carry)

    carry = lax.fori_loop(0, i, body, _softmax_init(t, V_HEAD))
    qc = lax.broadcasted_iota(jnp.int32, (t, t), 0) // CHUNK
    kc = lax.broadcasted_iota(jnp.int32, (t, t), 1) // CHUNK
    s = jnp.where(kc <= qc, scores(i), NEG_INF)
    _, l, acc = _online_softmax_step(s, v_ref[0, _tile(i, t), :], carry)
    o_ref[0] = (acc / l).astype(o_ref.dtype)


def _mla_prompt(qn, qr, kv, kr2):
    b, s, _ = qn.shape
    t = 512
    return pl.pallas_call(
        functools.partial(_mla_kernel, t=t), grid=(b, H_MLA, s // t),
        in_specs=[pl.BlockSpec((1, t, QK_NOPE), lambda bb, h, i: (bb, i, h)),
                  pl.BlockSpec((1, t, LANES), lambda bb, h, i: (bb, i, h // 2)),
                  pl.BlockSpec((1, s, QK_NOPE), lambda bb, h, i: (bb, 0, h)),
                  pl.BlockSpec((1, s, LANES), lambda bb, h, i: (bb, 0, 0)),
                  pl.BlockSpec((1, s, V_HEAD), lambda bb, h, i: (bb, 0, H_MLA + h))],
        out_specs=pl.BlockSpec((1, t, V_HEAD), lambda bb, h, i: (bb, i, h)),
        out_shape=jax.ShapeDtypeStruct((b, s, H_MLA * V_HEAD), BF16),
        compiler_params=_params(3), name="mla_prompt",
    )(qn, qr, kv, kr2, kv)


def _sample_attn_kernel(q_ref, kc_ref, vc_ref, kn_ref, vn_ref, x_ref, *rest, fox, n_cache, n_new, n_pad):
    if fox:
        fq_ref, o_ref, kall, vall = rest
    else:
        o_ref, kall, vall = rest
    n_all = n_cache + n_new
    kall[0:n_cache, :] = kc_ref[0].astype(BF16)
    vall[0:n_cache, :] = vc_ref[0].astype(BF16)
    kall[n_cache:n_all, :] = kn_ref[0]
    vall[n_cache:n_all, :] = vn_ref[0]
    kall[n_all:n_pad, :] = jnp.zeros((n_pad - n_all, HEAD_DIM), BF16)
    vall[n_all:n_pad, :] = jnp.zeros((n_pad - n_all, HEAD_DIM), BF16)
    s = lax.dot_general(q_ref[0], kall[...], NT_DIMS, preferred_element_type=F32) * ATT_SCALE
    kpos = lax.broadcasted_iota(jnp.int32, (n_new, n_pad), 1)
    if fox:
        qpos = n_cache + lax.broadcasted_iota(jnp.int32, (n_new, n_pad), 0)
        s = jnp.where(kpos <= qpos, s + (fq_ref[0] - x_ref[0]), NEG_INF)
    else:
        s = jnp.where(kpos < n_all, s + x_ref[0], NEG_INF)
    m = jnp.max(s, axis=1, keepdims=True)
    p = jnp.exp(s - m)
    l = jnp.sum(p, axis=1, keepdims=True)
    o = jnp.dot(p.astype(BF16), vall[...], preferred_element_type=F32) / l
    o_ref[0] = o.astype(o_ref.dtype)


def _sample_attn(zb, cache_k, cache_v, extra, fq, *, fox, col0, heads):
    db, dt, _ = zb.shape
    n_cache = cache_k.shape[1]
    n_pad = -(-(n_cache + dt) // LANES) * LANES
    head_spec = lambda off: pl.BlockSpec((1, dt, HEAD_DIM), lambda bb, h: (bb, 0, col0 + off + h))
    cache_spec = pl.BlockSpec((1, n_cache, HEAD_DIM), lambda bb, h: (bb, 0, h))
    ins = [zb, cache_k, cache_v, zb, zb, extra]
    specs = [head_spec(0), cache_spec, cache_spec, head_spec(heads), head_spec(2 * heads)]
    if fox:
        specs.append(pl.BlockSpec((1, 1, n_pad), lambda bb, h: (bb * heads + h, 0, 0)))
        ins.append(fq)
        specs.append(pl.BlockSpec((1, dt, 1), lambda bb, h: (bb * heads + h, 0, 0)))
    else:
        specs.append(pl.BlockSpec((1, dt, n_pad), lambda bb, h: (h, 0, 0)))
    return pl.pallas_call(
        functools.partial(_sample_attn_kernel, fox=fox, n_cache=n_cache, n_new=dt, n_pad=n_pad),
        grid=(db, heads), in_specs=specs,
        out_specs=pl.BlockSpec((1, dt, HEAD_DIM), lambda bb, h: (bb, 0, h)),
        out_shape=jax.ShapeDtypeStruct((db, dt, heads * HEAD_DIM), BF16),
        scratch_shapes=[pltpu.VMEM((n_pad, HEAD_DIM), BF16), pltpu.VMEM((n_pad, HEAD_DIM), BF16)],
        compiler_params=_params(2), name="fox_sample" if fox else "band_sample",
    )(*ins)


def _mla_sample_kernel(q_ref, cc_ref, ckr_ref, cn_ref, krn_ref, o_ref, kall, *, n_cache, n_new, n_pad):
    n_all = n_cache + n_new
    kall[...] = jnp.zeros(kall.shape, BF16)
    kall[0:n_cache, 0:KV_LORA] = cc_ref[0].astype(BF16)
    kall[0:n_cache, KV_LORA:KV_LORA + QK_ROPE] = ckr_ref[0].astype(BF16)
    kall[n_cache:n_all, 0:KV_LORA] = cn_ref[0]
    kall[n_cache:n_all, KV_LORA:KV_LORA + QK_ROPE] = krn_ref[0]
    q = q_ref[0]
    rows = q.shape[0]
    s = lax.dot_general(q, kall[...], NT_DIMS, preferred_element_type=F32) * MLA_SCALE
    kpos = lax.broadcasted_iota(jnp.int32, (rows, n_pad), 1)
    qpos = n_cache + lax.broadcasted_iota(jnp.int32, (rows, n_pad), 0) % n_new
    mask = (kpos // CHUNK <= qpos // CHUNK) & (kpos < n_all)
    s = jnp.where(mask, s, NEG_INF)
    m = jnp.max(s, axis=1, keepdims=True)
    p = jnp.exp(s - m)
    l = jnp.sum(p, axis=1, keepdims=True)
    o = jnp.dot(p.astype(BF16), kall[:, 0:KV_LORA], preferred_element_type=F32) / l
    o_ref[0] = o.astype(o_ref.dtype)


def _mla_sample(q_cat, cache_ckv, cache_kr, ckv_new, kr_new):
    db, rows, width = q_cat.shape
    n_cache = cache_ckv.shape[1]
    dt = ckv_new.shape[1]
    n_pad = -(-(n_cache + dt) // LANES) * LANES
    full = lambda a: pl.BlockSpec((1,) + a.shape[1:], lambda bb: (bb, 0, 0))
    return pl.pallas_call(
        functools.partial(_mla_sample_kernel, n_cache=n_cache, n_new=dt, n_pad=n_pad),
        grid=(db,), in_specs=[full(q_cat), full(cache_ckv), full(cache_kr), full(ckv_new), full(kr_new)],
        out_specs=pl.BlockSpec((1, rows, KV_LORA), lambda bb: (bb, 0, 0)),
        out_shape=jax.ShapeDtypeStruct((db, rows, KV_LORA), BF16),
        scratch_shapes=[pltpu.VMEM((n_pad, width), BF16)],
        compiler_params=_params(1), name="mla_sample",
    )(q_cat, cache_ckv, cache_kr, ckv_new, kr_new)


def _odd_post_kernel(z_ref, c_ref, s_ref, gq_ref, gkv_ref, cq_ref, ckv32_ref, ckv16_ref, kr32_ref, kr16_ref):
    def rms(x, g):
        return x * lax.rsqrt(jnp.mean(x * x, axis=-1, keepdims=True) + EPS) * g

    z = z_ref[0]
    cq_ref[0] = rms(z[:, 0:Q_LORA], gq_ref[...]).astype(BF16)
    ckv = rms(z[:, Q_LORA:Q_LORA + KV_LORA], gkv_ref[...])
    ckv32_ref[0] = ckv
    ckv16_ref[0] = ckv.astype(BF16)
    base = Q_LORA + KV_LORA
    kr = z[:, base:base + LANES] * c_ref[0] + z[:, base + LANES:base + 2 * LANES] * s_ref[0]
    kr32_ref[0] = kr
    kr16_ref[0] = kr.astype(BF16)


def _odd_post(z, cos2, sin2, g_q, g_kv):
    b, t, width = z.shape
    tm = min(512, t)
    row = lambda c: pl.BlockSpec((1, tm, c), lambda bb, i: (bb, i, 0))
    table = pl.BlockSpec((1, tm, LANES), lambda bb, i: (0, i, 0))
    vec = lambda c: pl.BlockSpec((1, c), lambda bb, i: (0, 0))
    shapes = [(Q_LORA, BF16), (KV_LORA, F32), (KV_LORA, BF16), (LANES, F32), (LANES, BF16)]
    return pl.pallas_call(
        _odd_post_kernel, grid=(b, t // tm),
        in_specs=[row(width), table, table, vec(Q_LORA), vec(KV_LORA)],
        out_specs=[row(c) for c, _ in shapes],
        out_shape=[jax.ShapeDtypeStruct((b, t, c), dt) for c, dt in shapes],
        compiler_params=_params(2), name="mla_latent_post",
    )(z, cos2, sin2, g_q.reshape(1, Q_LORA), g_kv.reshape(1, KV_LORA))


def _q_post_kernel(q_ref, c_ref, s_ref, qn_ref, qr_ref):
    n_nope = H_MLA * QK_NOPE
    n_rope = H_MLA * QK_ROPE
    reps = n_rope // LANES
    q = q_ref[0]
    qn_ref[0] = q[:, 0:n_nope].astype(BF16)
    cos = jnp.concatenate([c_ref[0]] * reps, axis=1)
    sin = jnp.concatenate([s_ref[0]] * reps, axis=1)
    qr_ref[0] = (q[:, n_nope:n_nope + n_rope] * cos + q[:, n_nope + n_rope:n_nope + 2 * n_rope] * sin).astype(BF16)


def _q_post(q, cos2, sin2):
    b, t, width = q.shape
    tm = min(512, t)
    row = lambda c: pl.BlockSpec((1, tm, c), lambda bb, i: (bb, i, 0))
    table = pl.BlockSpec((1, tm, LANES), lambda bb, i: (0, i, 0))
    return pl.pallas_call(
        _q_post_kernel, grid=(b, t // tm),
        in_specs=[row(width), table, table],
        out_specs=[row(H_MLA * QK_NOPE), row(H_MLA * QK_ROPE)],
        out_shape=[jax.ShapeDtypeStruct((b, t, H_MLA * QK_NOPE), BF16),
                   jax.ShapeDtypeStruct((b, t, H_MLA * QK_ROPE), BF16)],
        compiler_params=_params(2), name="mla_query_post",
    )(q, cos2, sin2)


PEER_RANKS = PEER_TOPK + 1


def _peer_route_kernel(q_ref, k1_ref, k2_ref, s2_ref, e2_ref, t1_ref, r1_ref, v1s, v2s, cand):
    def top_values(s, dst):
        cur = s
        for r in range(PEER_RANKS):
            m = jnp.max(cur, axis=0, keepdims=True)
            dst[r:r + 1, :] = m
            cur = jnp.where(cur == m, NEG_INF, cur)

    def kth_largest(x, k):
        cur = x
        for _ in range(k - 1):
            m = jnp.max(cur, axis=0, keepdims=True)
            cur = jnp.where(cur == m, NEG_INF, cur)
        return jnp.max(cur, axis=0, keepdims=True), cur

    q1 = q_ref[0, :, 0:PEER_HALF]
    q2 = q_ref[0, :, PEER_HALF:2 * PEER_HALF]
    s1 = lax.dot_general(k1_ref[0], q1, NT_DIMS, preferred_element_type=F32)
    s2 = lax.dot_general(k2_ref[0], q2, NT_DIMS, preferred_element_type=F32)
    top_values(s1, v1s)
    top_values(s2, v2s)
    for a in range(PEER_TOPK):
        cand[a * PEER_TOPK:(a + 1) * PEER_TOPK, :] = v1s[a:a + 1, :] + v2s[0:PEER_TOPK, :]
    c = cand[...]
    c16, rest = kth_largest(c, PEER_TOPK)
    rest = jnp.where(rest == c16, NEG_INF, rest)
    v1max, v2max = v1s[0:1, :], v2s[0:1, :]
    c17 = jnp.maximum(jnp.max(rest, axis=0, keepdims=True),
                      jnp.maximum(v1s[PEER_TOPK:PEER_RANKS, :] + v2max, v1max + v2s[PEER_TOPK:PEER_RANKS, :]))
    tau = 0.5 * (c16 + c17)
    z = jnp.sum(jnp.where(c >= c16, jnp.exp(c - (v1max + v2max)), 0.0), axis=0, keepdims=True)
    s2_ref[0, 0] = s2
    e2_ref[0, 0] = jnp.exp(s2 - v2max)
    t1_ref[0, 0] = tau - s1
    r1_ref[0, 0] = jnp.exp(s1 - v1max) / z


def _peer_route(q, key1, key2):
    b, t, _ = q.shape
    tm = min(512, t)
    out = jax.ShapeDtypeStruct((b, PEER_HEADS, N_KEYS, t), F32)
    out_spec = pl.BlockSpec((1, 1, N_KEYS, tm), lambda bb, i, h: (bb, h, 0, i))
    key_spec = pl.BlockSpec((1, N_KEYS, PEER_HALF), lambda bb, i, h: (h, 0, 0))
    return pl.pallas_call(
        _peer_route_kernel, grid=(b, t // tm, PEER_HEADS),
        in_specs=[pl.BlockSpec((1, tm, 2 * PEER_HALF), lambda bb, i, h: (bb, i, h)), key_spec, key_spec],
        out_specs=[out_spec] * 4, out_shape=[out] * 4,
        scratch_shapes=[pltpu.VMEM((PEER_RANKS + 7, tm), F32), pltpu.VMEM((PEER_RANKS + 7, tm), F32),
                        pltpu.VMEM((PEER_TOPK * PEER_TOPK, tm), F32)],
        compiler_params=_params(3), name="peer_route",
    )(q, key1, key2)


def _peer_dense_kernel(x_ref, u_ref, vt_ref, s2_ref, e2_ref, t1_ref, r1_ref, res_ref, gate_ref, o_ref, acc_ref,
                       *, te, n_e):
    e = pl.program_id(2)

    @pl.when(e == 0)
    def _():
        acc_ref[...] = jnp.zeros(acc_ref.shape, F32)

    ht = lax.dot_general(u_ref[...], x_ref[0], NT_DIMS, preferred_element_type=F32)
    act = 0.5 * ht * (1.0 + lax.erf(ht * np.float32(math.sqrt(0.5))))
    rows = te // N_KEYS
    parts = []
    for r in range(rows):
        i1 = e * rows + r
        w = None
        for h in range(PEER_HEADS):
            thr = t1_ref[0, h, pl.ds(i1, 1), :]
            r1 = r1_ref[0, h, pl.ds(i1, 1), :]
            c = jnp.where(s2_ref[0, h] >= thr, e2_ref[0, h] * r1, 0.0)
            w = c if w is None else w + c
        parts.append((w * act[r * N_KEYS:(r + 1) * N_KEYS, :]).astype(BF16))
    a_t = jnp.concatenate(parts, axis=0)
    acc_ref[...] += jnp.dot(vt_ref[...], a_t, preferred_element_type=F32)

    @pl.when(e == n_e - 1)
    def _():
        o_ref[0] = res_ref[0] + gate_ref[0] * acc_ref[...].T


def _peer_dense(h, u, vt, route, res, gate):
    b, t, d = h.shape
    n_exp = u.shape[0]
    tm = min(512, t)
    te = 512
    n_e = n_exp // te
    route_spec = pl.BlockSpec((1, PEER_HEADS, N_KEYS, tm), lambda bb, i, e: (bb, 0, 0, i))
    return pl.pallas_call(
        functools.partial(_peer_dense_kernel, te=te, n_e=n_e), grid=(b, t // tm, n_e),
        in_specs=[pl.BlockSpec((1, tm, d), lambda bb, i, e: (bb, i, 0)),
                  pl.BlockSpec((te, d), lambda bb, i, e: (e, 0)),
                  pl.BlockSpec((d, te), lambda bb, i, e: (0, e)),
                  route_spec, route_spec, route_spec, route_spec,
                  pl.BlockSpec((1, tm, d), lambda bb, i, e: (bb, i, 0), pipeline_mode=pl.Buffered(1)),
                  _row_spec(gate, tm)],
        out_specs=pl.BlockSpec((1, tm, d), lambda bb, i, e: (bb, i, 0)),
        out_shape=jax.ShapeDtypeStruct((b, t, d), F32),
        scratch_shapes=[pltpu.VMEM((d, tm), F32)],
        compiler_params=_params(3, PEER_VMEM_LIMIT_BYTES), name="peer_dense",
    )(h, u, vt, *route, res, gate)


def _peer(h, x_res, gate, w_q, key1, key2, u, vt):
    q = _mm([h], [w_q], out_dtypes=(BF16,), name="peer_query")
    route = _peer_route(q, key1, key2)
    return _peer_dense(h, u, vt, route, x_res, gate)


def _rope_tables(pos):
    half = QK_ROPE // 2
    freqs = ROPE_THETA ** (-jnp.arange(half, dtype=F32) / half)
    ang = pos.astype(F32)[:, None] * freqs
    cos, sin = jnp.cos(ang), jnp.sin(ang)
    cos2 = jnp.tile(cos, (1, LANES // half))
    sin2 = jnp.tile(jnp.concatenate([-sin, sin], axis=1), (1, LANES // QK_ROPE))
    return cos2, sin2


def _band_bias_tiles(table, t):
    ar = jnp.arange(t)
    d0 = ar[:, None] - ar[None, :]
    idx = jnp.clip(jnp.stack([d0, d0 + t]), -REL_CLIP, REL_CLIP) + REL_CLIP
    return jnp.take(table, idx, axis=1).astype(F32)


def _split_mod(m):
    d = m.shape[-1] // 6
    return [m[..., k * d:(k + 1) * d] for k in range(6)]


def kernel(x_prompt, x_sample, cache_fox_k, cache_fox_v, cache_fox_logf, cache_band_k, cache_band_v,
           cache_mla_ckv, cache_mla_krope, c_prompt, c_sample, w_ada, b_ada, g_mix, g_ffn, w_even_in,
           b_forget, rel_bias_table, w_even_out, w_odd_in, g_q_lora, g_kv_lora, w_uq, w_uk, w_uv,
           w_odd_out, w_peer_q, peer_key1, peer_key2, peer_u, peer_v, g_final):
    bsz, seq, d = x_prompt.shape
    db, dt, _ = x_sample.shape
    past = cache_fox_k.shape[2]
    depth = w_ada.shape[0]
    n_s = db * dt
    keep = min(BAND_KEEP, seq)
    a = H_FOX * HEAD_DIM

    n_c = bsz + db
    c_rows = -(-n_c // 16) * 16
    c_all = jnp.pad(jnp.concatenate([c_prompt, c_sample], axis=0), ((0, c_rows - n_c), (0, 0)))
    mod = _ada(c_all, w_ada, b_ada)

    xp = x_prompt
    xs = x_sample.reshape(1, n_s, d)
    pos_p = jnp.arange(seq)
    pos_s = jnp.tile(past + jnp.arange(dt), db)
    outs = {k: [] for k in ("pfk", "pfv", "pfl", "pbk", "pbv", "pck", "pkr",
                            "sfk", "sfv", "sfl", "sbk", "sbv", "sck", "skr")}

    for l in range(depth):
        mp = [m[:, None, :] for m in _split_mod(mod[l, :bsz])]
        ms = [jnp.repeat(m, dt, axis=0)[None] for m in _split_mod(mod[l, bsz:n_c])]
        hp = _norm(xp, g_mix[l], mp[0], mp[1])
        hs = _norm(xs, g_mix[l], ms[0], ms[1])
        if l % 2 == 0:
            e = l // 2
            w_in = w_even_in[e]
            w_main = jnp.concatenate([w_in[:, :3 * a], w_in[:, 3 * a + H_FOX:]], axis=1).astype(BF16)[None]
            w_f = jnp.pad(w_in[:, 3 * a:3 * a + H_FOX], ((0, 0), (0, LANES - H_FOX))).astype(BF16)
            b_f = jnp.pad(b_forget[e], (0, LANES - H_FOX)).reshape(1, LANES)
            w_out = w_even_out[e].astype(BF16)
            w_out_a, w_out_b = w_out[None, :a], w_out[None, a:]
            bias_tiles = _band_bias_tiles(rel_bias_table[e], BAND_KEEP)

            z32, z16 = _mm([hp], [w_main], out_dtypes=(F32, BF16), name="even_in")
            logf = _logf(hp, w_f, b_f)
            frow, _ = _cumsum_rows(jnp.swapaxes(logf, 1, 2), jnp.zeros((bsz, H_FOX, 1), F32))
            o_a = _fox_prompt(z16, frow.reshape(bsz * H_FOX, 1, seq), frow.reshape(bsz * H_FOX, seq, 1))
            o_b = _band_prompt(z16, bias_tiles)
            xp = _mm([o_a, o_b], [w_out_a, w_out_b], res=xp, gate=mp[2], name="even_out")
            heads = lambda x, n: x.reshape(x.shape[0], x.shape[1], n, HEAD_DIM)
            outs["pfk"].append(heads(z32[:, :, a:2 * a], H_FOX))
            outs["pfv"].append(heads(z32[:, :, 2 * a:3 * a], H_FOX))
            outs["pfl"].append(logf)
            outs["pbk"].append(heads(z32[:, seq - keep:, 4 * a:5 * a], H_BAND))
            outs["pbv"].append(heads(z32[:, seq - keep:, 5 * a:6 * a], H_BAND))

            z32, z16 = _mm([hs], [w_main], out_dtypes=(F32, BF16), name="even_in")
            logf = _logf(hs, w_f, b_f).reshape(db, dt, H_FOX)
            frow_c, c_end = _cumsum_rows(jnp.swapaxes(cache_fox_logf[e].astype(F32), 1, 2),
                                         jnp.zeros((db, H_FOX, 1), F32))
            n_pad = -(-(past + dt) // LANES) * LANES
            logf_t = jnp.pad(jnp.swapaxes(logf, 1, 2), ((0, 0), (0, 0), (0, LANES - dt)))
            frow_n, _ = _cumsum_rows(logf_t, c_end)
            frow_n = frow_n[:, :, :dt]
            frow_all = jnp.pad(jnp.concatenate([frow_c, frow_n], axis=2),
                               ((0, 0), (0, 0), (0, n_pad - past - dt))).reshape(db * H_FOX, 1, n_pad)
            zs = z16.reshape(db, dt, -1)
            o_a = _sample_attn(zs, cache_fox_k[e].reshape(db, past, a), cache_fox_v[e].reshape(db, past, a),
                               frow_all, frow_n.reshape(db * H_FOX, dt, 1), fox=True, col0=0, heads=H_FOX)
            n_band = cache_band_k.shape[2]
            n_pad_b = -(-(n_band + dt) // LANES) * LANES
            dist = (n_band + jnp.arange(dt))[:, None] - jnp.arange(n_pad_b)[None, :]
            bias_s = jnp.take(rel_bias_table[e], jnp.clip(dist, -REL_CLIP, REL_CLIP) + REL_CLIP, axis=1).astype(F32)
            o_b = _sample_attn(zs, cache_band_k[e].reshape(db, n_band, a), cache_band_v[e].reshape(db, n_band, a),
                               bias_s, None, fox=False, col0=3 * H_FOX, heads=H_BAND)
            xs = _mm([o_a.reshape(1, n_s, a), o_b.reshape(1, n_s, a)], [w_out_a, w_out_b], res=xs, gate=ms[2],
                     name="even_out")
            heads_s = lambda x, n: x.reshape(db, dt, n, HEAD_DIM)
            outs["sfk"].append(heads_s(z32[0, :, a:2 * a], H_FOX))
            outs["sfv"].append(heads_s(z32[0, :, 2 * a:3 * a], H_FOX))
            outs["sfl"].append(logf)
            outs["sbk"].append(heads_s(z32[0, :, 4 * a:5 * a], H_BAND))
            outs["sbv"].append(heads_s(z32[0, :, 5 * a:6 * a], H_BAND))
        else:
            o = l // 2
            half = QK_ROPE // 2
            swap = np.concatenate([np.arange(half, QK_ROPE), np.arange(half)])
            w_in = w_odd_in[o]
            w_kr = w_in[:, Q_LORA + KV_LORA:]
            w_odd = jnp.concatenate([w_in[:, :Q_LORA + KV_LORA], w_kr, w_kr, w_kr[:, swap], w_kr[:, swap]],
                                    axis=1).astype(BF16)[None]
            w_q3 = w_uq[o].reshape(Q_LORA, H_MLA, QK_NOPE + QK_ROPE)
            w_qr = w_q3[:, :, QK_NOPE:]
            w_q = jnp.concatenate([w_q3[:, :, :QK_NOPE].reshape(Q_LORA, -1), w_qr.reshape(Q_LORA, -1),
                                   w_qr[:, :, swap].reshape(Q_LORA, -1)], axis=1).astype(BF16)[None]
            w_ukv = jnp.concatenate([w_uk[o].reshape(KV_LORA, -1), w_uv[o].reshape(KV_LORA, -1)],
                                    axis=1).astype(BF16)[None]
            w_uk_t = jnp.transpose(w_uk[o], (1, 2, 0)).astype(BF16)
            w_uv_h = jnp.transpose(w_uv[o], (1, 0, 2)).astype(BF16)
            w_out = w_odd_out[o].astype(BF16)[None]

            cos2, sin2 = _rope_tables(pos_p)
            z = _mm([hp], [w_odd], name="odd_in")
            cq, ckv32, ckv16, kr32, kr16 = _odd_post(z, cos2[None], sin2[None], g_q_lora[o], g_kv_lora[o])
            q = _mm([cq], [w_q], name="mla_q_up")
            qn, qr = _q_post(q, cos2[None], sin2[None])
            kv = _mm([ckv16], [w_ukv], out_dtypes=(BF16,), name="mla_kv_up")
            o_p = _mla_prompt(qn, qr, kv, kr16)
            xp = _mm([o_p], [w_out], res=xp, gate=mp[2], name="odd_out")
            outs["pck"].append(ckv32)
            outs["pkr"].append(kr32[:, :, :QK_ROPE])

            cos2, sin2 = _rope_tables(pos_s)
            z = _mm([hs], [w_odd], name="odd_in")
            cq, ckv32, ckv16, kr32, kr16 = _odd_post(z, cos2[None], sin2[None], g_q_lora[o], g_kv_lora[o])
            q = _mm([cq], [w_q], name="mla_q_up")
            qn, qr = _q_post(q, cos2[None], sin2[None])
            qn_h = jnp.swapaxes(qn.reshape(n_s, H_MLA, QK_NOPE), 0, 1)
            q_lat = _mm([qn_h], [w_uk_t], out_dtypes=(BF16,), name="mla_q_absorb")
            by_batch = lambda x: jnp.swapaxes(x.reshape(H_MLA, db, dt, x.shape[-1]), 0, 1).reshape(db, H_MLA * dt, -1)
            qr_h = jnp.swapaxes(qr.reshape(n_s, H_MLA, QK_ROPE), 0, 1)
            q_cat = jnp.concatenate([by_batch(q_lat), by_batch(qr_h),
                                     jnp.zeros((db, H_MLA * dt, LANES - QK_ROPE), BF16)], axis=2)
            o_lat = _mla_sample(q_cat, cache_mla_ckv[o], cache_mla_krope[o], ckv16.reshape(db, dt, KV_LORA),
                                kr16.reshape(db, dt, LANES)[:, :, :QK_ROPE])
            o_lat = jnp.swapaxes(o_lat.reshape(db, H_MLA, dt, KV_LORA), 0, 1).reshape(H_MLA, n_s, KV_LORA)
            o_h = _mm([o_lat], [w_uv_h], out_dtypes=(BF16,), name="mla_v_up")
            o_s = jnp.swapaxes(o_h, 0, 1).reshape(1, n_s, H_MLA * V_HEAD)
            xs = _mm([o_s], [w_out], res=xs, gate=ms[2], name="odd_out")
            outs["sck"].append(ckv32.reshape(db, dt, KV_LORA))
            outs["skr"].append(kr32.reshape(db, dt, LANES)[:, :, :QK_ROPE])

        w_pq = w_peer_q[l].astype(BF16)[None]
        k1, k2 = peer_key1[l].astype(BF16), peer_key2[l].astype(BF16)
        u = peer_u[l].astype(BF16)
        vt = peer_v[l].T.astype(BF16)
        hp = _norm(xp, g_ffn[l], mp[3], mp[4])
        xp = _peer(hp, xp, mp[5], w_pq, k1, k2, u, vt)
        hs = _norm(xs, g_ffn[l], ms[3], ms[4])
        xs = _peer(hs, xs, ms[5], w_pq, k1, k2, u, vt)

    y_prompt = _norm(xp, g_final, out_dtype=F32)
    y_sample = _norm(xs, g_final, out_dtype=F32).reshape(db, dt, d)
    st = lambda k: jnp.stack(outs[k])
    return (y_prompt, y_sample, st("pfk"), st("pfv"), st("pfl"), st("pbk"), st("pbv"), st("pck"), st("pkr"),
            st("sfk"), st("sfv"), st("sfl"), st("sbk"), st("sbv"), st("sck"), st("skr"))
```
